```python
import math
import jax
import jax.numpy as jnp
from jax import lax
import numpy as np

D_MODEL = 2048
BATCH = 32
SEQ = 256
DEPTH = 1
DEC_BATCH = 4
DEC_SEQ = 2048
PAST_LEN = 512

GRID_W = 64
SSM_EXPAND = 2
D_SSM = SSM_EXPAND * D_MODEL
SSM_HEADDIM = 64
N_SSM_HEADS = D_SSM // SSM_HEADDIM
N_SSM_GROUPS = 8
D_STATE = 128
D_CONV = 3
CHUNK = 128
CONV_DIM = D_SSM + 2 * N_SSM_GROUPS * D_STATE
N_MLA_HEADS = 16
Q_LORA_RANK = 512
KV_LORA_RANK = 512
QK_NOPE_DIM = 128
QK_ROPE_DIM = 64
V_HEAD_DIM = 128
ROPE_THETA = 10000.0
Q_BLOCK = 128
N_EXPERTS = 32
TOP_K = 4
D_FF = D_MODEL
SWIGLU_LIMIT = 7.0
SWIGLU_ALPHA = 1.702
LN_EPS = 1e-5
RMS_EPS = 1e-6
DEEPNORM_ALPHA = (2 * DEPTH) ** 0.25
DEEPNORM_BETA = (8 * DEPTH) ** -0.25
IN_SPLITS = (D_SSM, CONV_DIM, 2 * N_SSM_HEADS, Q_LORA_RANK, KV_LORA_RANK + QK_ROPE_DIM, 2 * D_MODEL)
D_IN_PROJ = sum(IN_SPLITS)
IN_OFFSETS = tuple(sum(IN_SPLITS[:i + 1]) for i in range(len(IN_SPLITS) - 1))

kernel_name = 'hybrid_ssd_mla_moe_diffusion_step'


def layer_norm(x, w, b):
    xf = x.astype(jnp.float32)
    mu = jnp.mean(xf, -1, keepdims=True)
    var = jnp.mean(jnp.square(xf - mu), -1, keepdims=True)
    return ((xf - mu) * lax.rsqrt(var + LN_EPS) * w + b).astype(x.dtype)


def rms_norm(x, w):
    xf = x.astype(jnp.float32)
    return (xf * lax.rsqrt(jnp.mean(xf * xf, -1, keepdims=True) + RMS_EPS) * w).astype(x.dtype)


def adaln_mod(cond, lp):
    m = jax.nn.silu(cond) @ lp['w_ada'] + lp['b_ada']
    return jnp.split(m[:, None, :], 6, axis=-1)


def axial_rope(n_tok):
    rows = n_tok // GRID_W
    row = jnp.repeat(jnp.arange(rows, dtype=jnp.float32), GRID_W)
    col = jnp.tile(jnp.arange(GRID_W, dtype=jnp.float32), rows)
    n_freq = QK_ROPE_DIM // 4
    inv = ROPE_THETA ** (-jnp.arange(n_freq, dtype=jnp.float32) / n_freq)
    ang = jnp.concatenate([row[:, None] * inv, col[:, None] * inv], -1)
    return jnp.cos(ang), jnp.sin(ang)


def apply_rope(x, cos, sin):
    x1, x2 = jnp.split(x, 2, axis=-1)
    return jnp.concatenate([x1 * cos - x2 * sin, x2 * cos + x1 * sin], -1).astype(x.dtype)


def centred_dwconv(x, w, b):
    pad = (D_CONV - 1) // 2
    y = lax.conv_general_dilated(x, w[:, None, :], window_strides=(1,), padding=[(pad, pad)],
                                 dimension_numbers=('NWC', 'WIO', 'NWC'), feature_group_count=x.shape[-1])
    return y + b


def ssd_scan(x, dt, A, B, C, h0):
    b, L, H, P = x.shape
    G, N = B.shape[2], B.shape[3]
    hg = H // G
    nc = L // CHUNK
    xr = x.reshape(b, nc, CHUNK, G, hg, P)
    dtr = dt.reshape(b, nc, CHUNK, G, hg)
    Br = B.reshape(b, nc, CHUNK, G, N)
    Cr = C.reshape(b, nc, CHUNK, G, N)
    acum = jnp.cumsum(dtr * A.reshape(G, hg), axis=2)
    seg = acum[:, :, :, None] - acum[:, :, None]
    mask = jnp.tril(jnp.ones((CHUNK, CHUNK), dtype=bool))[:, :, None, None]
    lmat = jnp.exp(jnp.where(mask, seg, -jnp.inf))
    cb = jnp.einsum('bcqgn,bckgn->bcqkg', Cr, Br)
    m = cb[..., None] * lmat * dtr[:, :, None]
    y_diag = jnp.einsum('bcqkgh,bckghp->bcqghp', m, xr)
    decay_end = jnp.exp(acum[:, :, -1:] - acum)
    states = jnp.einsum('bckgn,bckghp->bcghpn', Br, xr * (decay_end * dtr)[..., None])
    chunk_decay = jnp.exp(acum[:, :, -1])

    def step(h, inp):
        s_c, d_c = inp
        return h * d_c[..., None, None] + s_c, h

    h_init = h0.astype(jnp.float32).reshape(b, G, hg, P, N)
    h_final, h_prev = lax.scan(step, h_init, (jnp.swapaxes(states, 0, 1), jnp.swapaxes(chunk_decay, 0, 1)))
    h_prev = jnp.swapaxes(h_prev, 0, 1)
    y_off = jnp.einsum('bcqgn,bcghpn->bcqghp', Cr, h_prev) * jnp.exp(acum)[..., None]
    y = (y_diag + y_off).reshape(b, L, H, P)
    return y, h_final.reshape(b, H, P, N)


def ssd_branch(z, xbc, dt_raw, lp, h_init):
    b, L, _ = z.shape
    xbc = jax.nn.silu(centred_dwconv(xbc, lp['conv_w'], lp['conv_b']))
    xs, B, C = jnp.split(xbc, [D_SSM, D_SSM + N_SSM_GROUPS * D_STATE], axis=-1)
    xs = xs.reshape(b, L, N_SSM_HEADS, SSM_HEADDIM)
    B = B.reshape(b, L, N_SSM_GROUPS, D_STATE)
    C = C.reshape(b, L, N_SSM_GROUPS, D_STATE)
    y = xs.astype(jnp.float32) * lp['d_skip'][:, None]
    finals = []
    for d in range(2):
        dt = jax.nn.softplus(dt_raw[..., d * N_SSM_HEADS:(d + 1) * N_SSM_HEADS].astype(jnp.float32) + lp['dt_bias'][d])
        A = -jnp.exp(lp['a_log'][d].astype(jnp.float32))
        if d == 0:
            y_d, h_d = ssd_scan(xs, dt, A, B, C, h_init[:, d])
        else:
            y_d, h_d = ssd_scan(jnp.flip(xs, 1), jnp.flip(dt, 1), A, jnp.flip(B, 1), jnp.flip(C, 1), h_init[:, d])
            y_d = jnp.flip(y_d, 1)
        y = y + y_d
        finals.append(h_d)
    yg = (y.reshape(b, L, D_SSM) * jax.nn.silu(z.astype(jnp.float32))).reshape(b, L, N_SSM_GROUPS, -1)
    yg = yg * lax.rsqrt(jnp.mean(yg * yg, -1, keepdims=True) + RMS_EPS)
    y = yg.reshape(b, L, D_SSM) * lp['ssm_norm_w']
    return y.astype(z.dtype), jnp.stack(finals, axis=1).astype(z.dtype)


def mla_expand_kv(ckv, k_rope, w_kv_b):
    b, L, _ = ckv.shape
    kv = (ckv @ w_kv_b).reshape(b, L, N_MLA_HEADS, QK_NOPE_DIM + V_HEAD_DIM)
    k_nope, v = kv[..., :QK_NOPE_DIM], kv[..., QK_NOPE_DIM:]
    k_r = jnp.broadcast_to(k_rope[:, :, None, :], (b, L, N_MLA_HEADS, QK_ROPE_DIM)).astype(k_nope.dtype)
    return jnp.concatenate([k_nope, k_r], -1), v


def block_attention(q, k, v):
    b, L, h, dq = q.shape
    nb = L // Q_BLOCK
    qb = jnp.swapaxes(q.reshape(b, nb, Q_BLOCK, h, dq), 0, 1)
    scale = dq ** -0.5

    def one(qblk):
        s = jnp.einsum('bqhd,bkhd->bhqk', qblk, k).astype(jnp.float32) * scale
        p = jax.nn.softmax(s, axis=-1).astype(v.dtype)
        return jnp.einsum('bhqk,bkhd->bqhd', p, v)

    o = lax.map(one, qb)
    return jnp.swapaxes(o, 0, 1).reshape(b, L, h, v.shape[-1])


def token_mixer(h, lp, rope, ctx_ckv, ctx_krope, ssm_init):
    b, L, _ = h.shape
    z, xbc, dt_raw, q_a, kv_a, gate_logits = jnp.split(h @ lp['w_in'], IN_OFFSETS, axis=-1)
    y_ssm, ssm_states = ssd_branch(z, xbc, dt_raw, lp, ssm_init)
    q = (rms_norm(q_a, lp['q_a_norm_w']) @ lp['w_q_b']).reshape(b, L, N_MLA_HEADS, QK_NOPE_DIM + QK_ROPE_DIM)
    q_nope, q_rope = q[..., :QK_NOPE_DIM], q[..., QK_NOPE_DIM:]
    ckv = rms_norm(kv_a[..., :KV_LORA_RANK], lp['kv_a_norm_w'])
    k_rope = kv_a[..., KV_LORA_RANK:]
    if rope is not None:
        cos, sin = rope
        q_rope = apply_rope(q_rope, cos[:, None, :], sin[:, None, :])
        k_rope_pos = apply_rope(k_rope, cos, sin)
    else:
        k_rope_pos = k_rope
    q = jnp.concatenate([q_nope, q_rope], -1)
    k, v = mla_expand_kv(ckv, k_rope_pos, lp['w_kv_b'])
    if ctx_ckv is not None:
        k_c, v_c = mla_expand_kv(ctx_ckv, ctx_krope, lp['w_kv_b'])
        k = jnp.concatenate([k, k_c.astype(k.dtype)], axis=1)
        v = jnp.concatenate([v, v_c.astype(v.dtype)], axis=1)
    o = block_attention(q, k, v).reshape(b, L, N_MLA_HEADS * V_HEAD_DIM)
    g_ssm, g_mla = jnp.split(jax.nn.sigmoid(gate_logits), 2, axis=-1)
    merged = g_ssm * (y_ssm @ lp['w_ssm_out']) + g_mla * (o @ lp['w_mla_out'])
    return (merged @ lp['w_o']).astype(h.dtype), ckv, k_rope, ssm_states


def moe_ffn(h, lp):
    b, L, D = h.shape
    t = h.reshape(b * L, D)
    logits = (t @ lp['router_w'] + lp['router_b']).astype(jnp.float32)
    top_v, top_i = lax.top_k(logits, TOP_K)
    wts = jax.nn.softmax(top_v, axis=-1)
    combine = jnp.sum(jax.nn.one_hot(top_i, N_EXPERTS, dtype=jnp.float32) * wts[..., None], axis=1)
    out = jnp.zeros((b * L, D), jnp.float32)
    for e in range(N_EXPERTS):
        gu = t @ lp['w_gu'][e] + lp['b_gu'][e]
        gate, up = gu[..., 0::2], gu[..., 1::2]
        gate = jnp.minimum(gate, SWIGLU_LIMIT)
        up = jnp.clip(up, -SWIGLU_LIMIT, SWIGLU_LIMIT)
        act = gate * jax.nn.sigmoid(SWIGLU_ALPHA * gate) * (up + 1.0)
        out = out + combine[:, e:e + 1] * (act @ lp['w_down'][e] + lp['b_down'][e])
    return out.reshape(b, L, D).astype(h.dtype)


def trunk_layer(x, cond, lp, rope, ctx_ckv, ctx_krope, ssm_init):
    sh1, sc1, g1, sh2, sc2, g2 = adaln_mod(cond, lp)
    mix, ckv, k_rope, states = token_mixer(x * (1.0 + sc1) + sh1, lp, rope, ctx_ckv, ctx_krope, ssm_init)
    x = layer_norm(DEEPNORM_ALPHA * x + g1 * mix, lp['ln1_w'], lp['ln1_b'])
    x = layer_norm(DEEPNORM_ALPHA * x + g2 * moe_ffn(x * (1.0 + sc2) + sh2, lp), lp['ln2_w'], lp['ln2_b'])
    return x, ckv, k_rope, states


def setup_inputs(seed: int = 0) -> dict:
    key = jax.random.key(seed)
    ks = iter(jax.random.split(key, 48))

    def nrm(shape, scale=1.0):
        return jax.random.normal(next(ks), shape, jnp.float32) * scale

    L, D, H, E = DEPTH, D_MODEL, N_SSM_HEADS, N_EXPERTS
    dt0 = jnp.exp(jax.random.uniform(next(ks), (L, 2, H), jnp.float32, math.log(1e-3), math.log(1e-1)))
    dt_bias = dt0 + jnp.log(-jnp.expm1(-dt0))
    a_log = jnp.log(jax.random.uniform(next(ks), (L, 2, H), jnp.float32, 1.0, 16.0))
    return {
        'x_prompt': nrm((BATCH, SEQ, D)),
        'x_sample': nrm((DEC_BATCH, DEC_SEQ, D)),
        'cache_mla_ckv': nrm((DEC_BATCH, DEPTH, PAST_LEN, KV_LORA_RANK)),
        'cache_mla_krope': nrm((DEC_BATCH, DEPTH, PAST_LEN, QK_ROPE_DIM)),
        'state_ssm': nrm((DEC_BATCH, DEPTH, 2, H, SSM_HEADDIM, D_STATE), 0.5),
        'c': nrm((DEC_BATCH, D)),
        'c_ctx': nrm((D,)),
        'w_ada': nrm((L, D, 6 * D), 0.5 * D ** -0.5),
        'b_ada': nrm((L, 6 * D), 0.02),
        'w_in': nrm((L, D, D_IN_PROJ), D ** -0.5),
        'conv_w': nrm((L, D_CONV, CONV_DIM), D_CONV ** -0.5),
        'conv_b': nrm((L, CONV_DIM), 0.02),
        'a_log': a_log,
        'dt_bias': dt_bias,
        'd_skip': 1.0 + nrm((L, H), 0.1),
        'ssm_norm_w': 1.0 + nrm((L, D_SSM), 0.1),
        'w_ssm_out': nrm((L, D_SSM, D), D_SSM ** -0.5),
        'q_a_norm_w': 1.0 + nrm((L, Q_LORA_RANK), 0.1),
        'w_q_b': nrm((L, Q_LORA_RANK, N_MLA_HEADS * (QK_NOPE_DIM + QK_ROPE_DIM)), Q_LORA_RANK ** -0.5),
        'kv_a_norm_w': 1.0 + nrm((L, KV_LORA_RANK), 0.1),
        'w_kv_b': nrm((L, KV_LORA_RANK, N_MLA_HEADS * (QK_NOPE_DIM + V_HEAD_DIM)), KV_LORA_RANK ** -0.5),
        'w_mla_out': nrm((L, N_MLA_HEADS * V_HEAD_DIM, D), (N_MLA_HEADS * V_HEAD_DIM) ** -0.5),
        'w_o': nrm((L, D, D), DEEPNORM_BETA * D ** -0.5),
        'ln1_w': 1.0 + nrm((L, D), 0.1),
        'ln1_b': nrm((L, D), 0.02),
        'router_w': nrm((L, D, E), D ** -0.5),
        'router_b': nrm((L, E), 0.01),
        'w_gu': nrm((L, E, D, 2 * D_FF), D ** -0.5),
        'b_gu': nrm((L, E, 2 * D_FF), 0.01),
        'w_down': nrm((L, E, D_FF, D), DEEPNORM_BETA * D_FF ** -0.5),
        'b_down': nrm((L, E, D), 0.01),
        'ln2_w': 1.0 + nrm((L, D), 0.1),
        'ln2_b': nrm((L, D), 0.02),
    }


def reference(x_prompt, x_sample, cache_mla_ckv, cache_mla_krope, state_ssm, c, c_ctx,
              w_ada, b_ada, w_in, conv_w, conv_b, a_log, dt_bias, d_skip, ssm_norm_w, w_ssm_out,
              q_a_norm_w, w_q_b, kv_a_norm_w, w_kv_b, w_mla_out, w_o, ln1_w, ln1_b,
              router_w, router_b, w_gu, b_gu, w_down, b_down, ln2_w, ln2_b):
    rope = axial_rope(x_sample.shape[1])
    zero_state = jnp.zeros((x_prompt.shape[0], 2, N_SSM_HEADS, SSM_HEADDIM, D_STATE), x_prompt.dtype)
    y_p, y_s = x_prompt, x_sample
    ckv_list, krope_list, state_list = [], [], []
    for l in range(DEPTH):
        lp = {
            'w_ada': w_ada[l], 'b_ada': b_ada[l], 'w_in': w_in[l], 'conv_w': conv_w[l], 'conv_b': conv_b[l],
            'a_log': a_log[l], 'dt_bias': dt_bias[l], 'd_skip': d_skip[l], 'ssm_norm_w': ssm_norm_w[l],
            'w_ssm_out': w_ssm_out[l], 'q_a_norm_w': q_a_norm_w[l], 'w_q_b': w_q_b[l],
            'kv_a_norm_w': kv_a_norm_w[l], 'w_kv_b': w_kv_b[l], 'w_mla_out': w_mla_out[l], 'w_o': w_o[l],
            'ln1_w': ln1_w[l], 'ln1_b': ln1_b[l], 'router_w': router_w[l], 'router_b': router_b[l],
            'w_gu': w_gu[l], 'b_gu': b_gu[l], 'w_down': w_down[l], 'b_down': b_down[l],
            'ln2_w': ln2_w[l], 'ln2_b': ln2_b[l],
        }
        y_p, ckv_l, krope_l, state_l = trunk_layer(y_p, c_ctx[None, :], lp, None, None, None, zero_state)
        ckv_list.append(ckv_l)
        krope_list.append(krope_l)
        state_list.append(state_l)
        y_s, _, _, _ = trunk_layer(y_s, c, lp, rope, cache_mla_ckv[:, l], cache_mla_krope[:, l], state_ssm[:, l])
    new_cache_mla_ckv = jnp.stack(ckv_list, axis=1)
    new_cache_mla_krope = jnp.stack(krope_list, axis=1)
    new_state_ssm = jnp.stack(state_list, axis=1)
    return (y_p, y_s, new_cache_mla_ckv, new_cache_mla_krope, new_state_ssm)
```

```python
import functools
import math

import numpy as np
import jax
import jax.numpy as jnp
from jax import lax
from jax.experimental import pallas as pl
from jax.experimental.pallas import tpu as pltpu

F32 = jnp.float32
BF16 = jnp.bfloat16
I32 = jnp.int32

D_MODEL = 2048
D_SSM = 4096
SSM_HEADDIM = 64
N_SSM_HEADS = 64
N_SSM_GROUPS = 8
HEADS_PER_GROUP = N_SSM_HEADS // N_SSM_GROUPS
D_STATE = 128
D_CONV = 3
CHUNK = 128
GROUP_W = HEADS_PER_GROUP * SSM_HEADDIM
N_MLA_HEADS = 16
Q_LORA_RANK = 512
KV_LORA_RANK = 512
QK_NOPE_DIM = 128
QK_ROPE_DIM = 64
V_HEAD_DIM = 128
ROPE_THETA = 10000.0
GRID_W = 64
N_EXPERTS = 32
TOP_K = 4
D_FF = 2048
SWIGLU_LIMIT = 7.0
SWIGLU_ALPHA = 1.702
LN_EPS = 1e-5
RMS_EPS = 1e-6
DEPTH = 1
DEEPNORM_ALPHA = (2 * DEPTH) ** 0.25

C_Z = 0
C_XS = 4096
C_BC = 8192
C_QA = 10240
C_CKV = 10752
C_KR = 11264
C_DT = 11392
C_GS = 12288
C_GM = 14336
N_PROJ = 16384

LANES = 128
VMEM_LIMIT = 52 * 1024 * 1024


def _cp(sem, vmem=VMEM_LIMIT):
    return pltpu.CompilerParams(dimension_semantics=sem, vmem_limit_bytes=vmem)


def _silu(x):
    return x * jax.nn.sigmoid(x)


def _adaln_kernel(c_ref, w_ref, b_ref, o_ref):
    s = _silu(c_ref[...]).astype(BF16)
    o_ref[...] = jnp.dot(s, w_ref[...].astype(BF16), preferred_element_type=F32) + b_ref[...]


def _adaln(cond8, w_ada, b_ada):
    n = w_ada.shape[1]
    tn = 1024
    return pl.pallas_call(
        _adaln_kernel,
        grid=(n // tn,),
        in_specs=[pl.BlockSpec((8, D_MODEL), lambda j: (0, 0)),
                  pl.BlockSpec((D_MODEL, tn), lambda j: (0, j)),
                  pl.BlockSpec((1, tn), lambda j: (0, j))],
        out_specs=pl.BlockSpec((8, tn), lambda j: (0, j)),
        out_shape=jax.ShapeDtypeStruct((8, n), F32),
        compiler_params=_cp(("parallel",)),
        name="adaln",
    )(cond8, w_ada, b_ada)


def _inproj_kernel(x_ref, sc_ref, sh_ref, w_ref, o_ref, h_scr):
    @pl.when(pl.program_id(1) == 0)
    def _():
        h_scr[...] = (x_ref[...] * (1.0 + sc_ref[0]) + sh_ref[0]).astype(BF16)

    o_ref[...] = jnp.dot(h_scr[...], w_ref[...], preferred_element_type=F32)


def _mod_row(i, tm, t_ctx, l_lat):
    r0 = i * tm
    return jnp.where(r0 < t_ctx, 0, 1 + (r0 - t_ctx) // l_lat)


def _inproj(x, mods3, w_r, t_ctx, l_lat):
    t = x.shape[0]
    tm, tn = math.gcd(1024, t_ctx, l_lat), 1024
    row = functools.partial(_mod_row, tm=tm, t_ctx=t_ctx, l_lat=l_lat)
    return pl.pallas_call(
        _inproj_kernel,
        grid=(t // tm, N_PROJ // tn),
        in_specs=[pl.BlockSpec((tm, D_MODEL), lambda i, j: (i, 0)),
                  pl.BlockSpec((1, 1, D_MODEL), lambda i, j: (row(i), 0, 1)),
                  pl.BlockSpec((1, 1, D_MODEL), lambda i, j: (row(i), 0, 0)),
                  pl.BlockSpec((D_MODEL, tn), lambda i, j: (0, j))],
        out_specs=pl.BlockSpec((tm, tn), lambda i, j: (i, j)),
        out_shape=jax.ShapeDtypeStruct((t, N_PROJ), F32),
        scratch_shapes=[pltpu.VMEM((tm, D_MODEL), BF16)],
        compiler_params=_cp(("parallel", "arbitrary")),
        name="inproj",
    )(x, mods3, mods3, w_r)


def _conv_kernel(x_ref, xp_ref, xn_ref, b_ref_, bp_ref, bn_ref, wx_ref, bx_ref, wb_ref, bb_ref,
                 ox_ref, ob_ref, sx, sb, *, tc, t_ctx, l_ctx, l_lat):
    i = pl.program_id(0)
    r0 = i * tc
    in_ctx = r0 < t_ctx
    off = jnp.where(in_ctx, r0 % l_ctx, (r0 - t_ctx) % l_lat)
    seq_len = jnp.where(in_ctx, l_ctx, l_lat)
    has_prev = (off != 0).astype(F32)
    has_next = (off + tc != seq_len).astype(F32)

    def conv(x_r, p_r, n_r, w_r, bias_r, o_r, s):
        s[8:8 + tc, :] = x_r[...]
        s[7:8, :] = p_r[7:8, :] * has_prev
        s[8 + tc:9 + tc, :] = n_r[0:1, :] * has_next
        y = (s[7:7 + tc, :] * w_r[0:1, :] + s[8:8 + tc, :] * w_r[1:2, :]
             + s[9:9 + tc, :] * w_r[2:3, :] + bias_r[...])
        o_r[...] = _silu(y)

    conv(x_ref, xp_ref, xn_ref, wx_ref, bx_ref, ox_ref, sx)
    conv(b_ref_, bp_ref, bn_ref, wb_ref, bb_ref, ob_ref, sb)


def _conv(proj, conv_w, conv_b, t_ctx, l_ctx, l_lat):
    t = proj.shape[0]
    tc = 256
    nb8 = t // 8
    wx, wb = conv_w[:, :D_SSM], conv_w[:, D_SSM:]
    bx, bb = conv_b[None, :D_SSM], conv_b[None, D_SSM:]
    wbc = 2 * N_SSM_GROUPS * D_STATE
    cx, cb = C_XS // D_SSM, C_BC // wbc

    def prev_map(c):
        return lambda i: (jnp.maximum(i * (tc // 8) - 1, 0), c)

    def next_map(c):
        return lambda i: (jnp.minimum((i + 1) * (tc // 8), nb8 - 1), c)

    kern = functools.partial(_conv_kernel, tc=tc, t_ctx=t_ctx, l_ctx=l_ctx, l_lat=l_lat)
    return pl.pallas_call(
        kern,
        grid=(t // tc,),
        in_specs=[pl.BlockSpec((tc, D_SSM), lambda i: (i, cx)),
                  pl.BlockSpec((8, D_SSM), prev_map(cx)),
                  pl.BlockSpec((8, D_SSM), next_map(cx)),
                  pl.BlockSpec((tc, wbc), lambda i: (i, cb)),
                  pl.BlockSpec((8, wbc), prev_map(cb)),
                  pl.BlockSpec((8, wbc), next_map(cb)),
                  pl.BlockSpec((D_CONV, D_SSM), lambda i: (0, 0)),
                  pl.BlockSpec((1, D_SSM), lambda i: (0, 0)),
                  pl.BlockSpec((D_CONV, wbc), lambda i: (0, 0)),
                  pl.BlockSpec((1, wbc), lambda i: (0, 0))],
        out_specs=[pl.BlockSpec((tc, D_SSM), lambda i: (i, 0)),
                   pl.BlockSpec((tc, wbc), lambda i: (i, 0))],
        out_shape=[jax.ShapeDtypeStruct((t, D_SSM), F32), jax.ShapeDtypeStruct((t, wbc), F32)],
        scratch_shapes=[pltpu.VMEM((tc + 16, D_SSM), F32), pltpu.VMEM((tc + 16, wbc), F32)],
        compiler_params=_cp(("parallel",)),
        name="dwconv",
    )(proj, proj, proj, proj, proj, proj, wx, bx, wb, bb)


GPS = 2
N_GSTEPS = N_SSM_GROUPS // GPS
DT_LANES_PER_STEP = 2 * HEADS_PER_GROUP * GPS


def _ssd_aux(dt_raw, bias, a_row, gi):
    shift = (LANES - DT_LANES_PER_STEP * gi) % LANES
    x = pltpu.roll(dt_raw + bias, shift, 1)
    a = pltpu.roll(jnp.broadcast_to(a_row, dt_raw.shape), shift, 1)
    dt = jnp.maximum(x, 0.0) + jnp.log1p(jnp.exp(-jnp.abs(x)))
    dta = dt * a
    q = dta.shape[0]
    row = lax.broadcasted_iota(I32, dta.shape, 0)
    lane = lax.broadcasted_iota(I32, dta.shape, 1)
    p = dta
    s = 1
    while s < q:
        p = p + jnp.where(row >= s, pltpu.roll(p, s, 0), 0.0)
        s *= 2
    total = p[q - 1:q, :]
    suffix = total - p + dta
    fwd = (lane % (2 * HEADS_PER_GROUP)) < HEADS_PER_GROUP
    e = jnp.where(fwd, p, suffix)
    e_end = total
    return dt, e, jnp.exp(e), jnp.exp(e_end - e) * dt, jnp.exp(e_end)


def _pair_cols(m, l0, l1, lane_lo):
    return jnp.where(lane_lo, m[:, l0:l0 + 1], m[:, l1:l1 + 1])


def _ssd_bwd_kernel(xs_ref, b_ref, c_ref, dt_ref, bias_ref, a_ref, h0_ref, yoff_ref, hfin_ref, h_scr, *, nc):
    gi = pl.program_id(1)
    c = pl.program_id(2)

    @pl.when(c == 0)
    def _():
        for gl in range(GPS):
            h_scr[gl] = h0_ref[0, gl * GROUP_W:(gl + 1) * GROUP_W, :].T

    _, _, e_exp, wgt, cdec = _ssd_aux(dt_ref[...], bias_ref[...], a_ref[...], gi)
    lane_lo = lax.broadcasted_iota(I32, (CHUNK, LANES), 1) < SSM_HEADDIM
    lane_lo1 = lane_lo[0:1, :]
    for gl in range(GPS):
        cb16 = c_ref[:, gl * D_STATE:(gl + 1) * D_STATE].astype(BF16)
        bt16 = b_ref[:, gl * D_STATE:(gl + 1) * D_STATE].T.astype(BF16)
        hg = h_scr[gl]
        yo = jnp.dot(cb16, hg.astype(BF16), preferred_element_type=F32)
        xw, cd = [], []
        for pr in range(HEADS_PER_GROUP // 2):
            l0 = 2 * HEADS_PER_GROUP * gl + HEADS_PER_GROUP + 2 * pr
            l1 = l0 + 1
            lo = gl * GROUP_W + pr * LANES
            yoff_ref[:, lo:lo + LANES] = yo[:, pr * LANES:(pr + 1) * LANES] * _pair_cols(e_exp, l0, l1, lane_lo)
            xw.append((xs_ref[:, lo:lo + LANES] * _pair_cols(wgt, l0, l1, lane_lo)).astype(BF16))
            cd.append(jnp.where(lane_lo1, cdec[:, l0:l0 + 1], cdec[:, l1:l1 + 1]))
        upd = jnp.dot(bt16, jnp.concatenate(xw, axis=1), preferred_element_type=F32)
        h_scr[gl] = hg * jnp.concatenate(cd, axis=1) + upd

    @pl.when(c == nc - 1)
    def _():
        for gl in range(GPS):
            hfin_ref[0, gl * GROUP_W:(gl + 1) * GROUP_W, :] = h_scr[gl].T


def _ssd_fwd_kernel(xs_ref, b_ref, c_ref, dt_ref, bias_ref, a_ref, h0_ref, yoffb_ref, z_ref, dskip_ref, nw_ref,
                    y_ref, hfin_ref, h_scr, *, nc):
    gi = pl.program_id(1)
    c = pl.program_id(2)

    @pl.when(c == 0)
    def _():
        for gl in range(GPS):
            h_scr[gl] = h0_ref[0, gl * GROUP_W:(gl + 1) * GROUP_W, :].T

    dt, e, e_exp, wgt, cdec = _ssd_aux(dt_ref[...], bias_ref[...], a_ref[...], gi)
    e_t = e.T
    dt_t = dt.T
    row_i = lax.broadcasted_iota(I32, (CHUNK, CHUNK), 0)
    col_i = lax.broadcasted_iota(I32, (CHUNK, CHUNK), 1)
    tril = col_i <= row_i
    triu = col_i >= row_i
    lane_lo = col_i < SSM_HEADDIM
    lane_lo1 = lane_lo[0:1, :]
    lane_lo16 = lax.broadcasted_iota(I32, (2 * CHUNK, LANES), 1) < SSM_HEADDIM
    top_half = lax.broadcasted_iota(I32, (2 * CHUNK, LANES), 0) < CHUNK
    diag_mask = lane_lo16 == top_half
    neg_inf = -jnp.inf

    def head_m(cbm, lf):
        lb = lf + HEADS_PER_GROUP
        wf = jnp.exp(jnp.where(tril, e[:, lf:lf + 1] - e_t[lf:lf + 1, :], neg_inf)) * dt_t[lf:lf + 1, :]
        wb = jnp.exp(jnp.where(triu, e[:, lb:lb + 1] - e_t[lb:lb + 1, :], neg_inf)) * dt_t[lb:lb + 1, :]
        return (cbm * (wf + wb)).astype(BF16)

    for gl in range(GPS):
        cb16 = c_ref[:, gl * D_STATE:(gl + 1) * D_STATE].astype(BF16)
        b32 = b_ref[:, gl * D_STATE:(gl + 1) * D_STATE]
        b16 = b32.astype(BF16)
        bt16 = b32.T.astype(BF16)
        cbm = lax.dot_general(cb16, b16, (((1,), (1,)), ((), ())), preferred_element_type=F32)
        hg = h_scr[gl]
        yo = jnp.dot(cb16, hg.astype(BF16), preferred_element_type=F32)
        xw, cd, ys = [], [], []
        for pr in range(HEADS_PER_GROUP // 2):
            lf0 = 2 * HEADS_PER_GROUP * gl + 2 * pr
            lf1 = lf0 + 1
            lo = gl * GROUP_W + pr * LANES
            xp = xs_ref[:, lo:lo + LANES]
            xp16 = xp.astype(BF16)
            mcat = jnp.concatenate([head_m(cbm, lf0), head_m(cbm, lf1)], axis=1)
            xbd = jnp.where(diag_mask, jnp.concatenate([xp16, xp16], axis=0), jnp.zeros((), BF16))
            yd = jnp.dot(mcat, xbd, preferred_element_type=F32)
            y = (xp * dskip_ref[:, lo:lo + LANES] + yd
                 + yo[:, pr * LANES:(pr + 1) * LANES] * _pair_cols(e_exp, lf0, lf1, lane_lo)
                 + yoffb_ref[:, lo:lo + LANES])
            ys.append(y)
            xw.append((xp * _pair_cols(wgt, lf0, lf1, lane_lo)).astype(BF16))
            cd.append(jnp.where(lane_lo1, cdec[:, lf0:lf0 + 1], cdec[:, lf1:lf1 + 1]))
        upd = jnp.dot(bt16, jnp.concatenate(xw, axis=1), preferred_element_type=F32)
        h_scr[gl] = hg * jnp.concatenate(cd, axis=1) + upd
        yg = jnp.concatenate(ys, axis=1) * _silu(z_ref[:, gl * GROUP_W:(gl + 1) * GROUP_W])
        ms = jnp.mean(yg * yg, axis=-1, keepdims=True)
        y_ref[:, gl * GROUP_W:(gl + 1) * GROUP_W] = (
            yg * lax.rsqrt(ms + RMS_EPS) * nw_ref[:, gl * GROUP_W:(gl + 1) * GROUP_W]).astype(BF16)

    @pl.when(c == nc - 1)
    def _():
        for gl in range(GPS):
            hfin_ref[0, gl * GROUP_W:(gl + 1) * GROUP_W, :] = h_scr[gl].T


def _ssd(xs_c, bc_c, proj, dt_bias_r, a_r, h0, d_skip_r, norm_w, row0, n_seq, seq_len):
    nc = seq_len // CHUNK
    cb0 = row0 // CHUNK
    w = GPS * GROUP_W
    nbc = N_SSM_GROUPS // GPS
    t_loc = n_seq * seq_len
    grid = (n_seq, N_GSTEPS, nc)
    h0f, h0b = h0[:, 0], h0[:, 1]

    def specs(rev):
        def ch(c):
            return (nc - 1 - c) if rev else c
        return dict(
            xs=pl.BlockSpec((CHUNK, w), lambda s, g, c: (cb0 + s * nc + ch(c), g)),
            b=pl.BlockSpec((CHUNK, GPS * D_STATE), lambda s, g, c: (cb0 + s * nc + ch(c), g)),
            c=pl.BlockSpec((CHUNK, GPS * D_STATE), lambda s, g, c: (cb0 + s * nc + ch(c), nbc + g)),
            dt=pl.BlockSpec((CHUNK, LANES), lambda s, g, c: (cb0 + s * nc + ch(c), C_DT // LANES)),
            vec=pl.BlockSpec((1, LANES), lambda s, g, c: (0, 0)),
            h=pl.BlockSpec((1, w, D_STATE), lambda s, g, c: (s, g, 0)),
            loc=pl.BlockSpec((CHUNK, w), lambda s, g, c: (s * nc + ch(c), g)),
            z=pl.BlockSpec((CHUNK, w), lambda s, g, c: (cb0 + s * nc + ch(c), C_Z // w + g)),
            chan=pl.BlockSpec((1, w), lambda s, g, c: (0, g)),
        )

    sb = specs(True)
    yoff_b, hfin_b = pl.pallas_call(
        functools.partial(_ssd_bwd_kernel, nc=nc),
        grid=grid,
        in_specs=[sb["xs"], sb["b"], sb["c"], sb["dt"], sb["vec"], sb["vec"], sb["h"]],
        out_specs=[sb["loc"], sb["h"]],
        out_shape=[jax.ShapeDtypeStruct((t_loc, D_SSM), F32),
                   jax.ShapeDtypeStruct((n_seq, D_SSM, D_STATE), F32)],
        scratch_shapes=[pltpu.VMEM((GPS, D_STATE, GROUP_W), F32)],
        compiler_params=_cp(("parallel", "parallel", "arbitrary")),
        name="ssd_bwd",
    )(xs_c, bc_c, bc_c, proj, dt_bias_r, a_r, h0b)

    sf = specs(False)
    y, hfin_f = pl.pallas_call(
        functools.partial(_ssd_fwd_kernel, nc=nc),
        grid=grid,
        in_specs=[sf["xs"], sf["b"], sf["c"], sf["dt"], sf["vec"], sf["vec"], sf["h"], sf["loc"], sf["z"],
                  sf["chan"], sf["chan"]],
        out_specs=[sf["loc"], sf["h"]],
        out_shape=[jax.ShapeDtypeStruct((t_loc, D_SSM), BF16),
                   jax.ShapeDtypeStruct((n_seq, D_SSM, D_STATE), F32)],
        scratch_shapes=[pltpu.VMEM((GPS, D_STATE, GROUP_W), F32)],
        compiler_params=_cp(("parallel", "parallel", "arbitrary")),
        name="ssd_fwd",
    )(xs_c, bc_c, bc_c, proj, dt_bias_r, a_r, h0f, yoff_b, proj, d_skip_r, norm_w)
    return y, jnp.stack([hfin_f, hfin_b], axis=1)


def _rms(x, w):
    return x * lax.rsqrt(jnp.mean(x * x, axis=-1, keepdims=True) + RMS_EPS) * w


def _q_kernel(qa_ref, nw_ref, w_ref, cos_ref, sin_ref, qn_ref, qr_ref, *, use_rope):
    qn = _rms(qa_ref[...], nw_ref[...]).astype(BF16)
    q = jnp.dot(qn, w_ref[...], preferred_element_type=F32)
    nn = N_MLA_HEADS * QK_NOPE_DIM
    nr = N_MLA_HEADS * QK_ROPE_DIM
    qn_ref[...] = q[:, :nn].astype(BF16)
    qr = q[:, nn:nn + nr]
    if use_rope:
        reps = nr // LANES
        cos = jnp.concatenate([cos_ref[...]] * reps, axis=1)
        sin = jnp.concatenate([sin_ref[...]] * reps, axis=1)
        qr = qr * cos + q[:, nn + nr:] * sin
    for h in range(N_MLA_HEADS):
        qr_ref[h] = qr[:, h * QK_ROPE_DIM:(h + 1) * QK_ROPE_DIM].astype(BF16)


def _q_proj(proj, nw, w_q, cos2, sin2, row0, n_rows, use_rope):
    tm = 512
    rb0 = row0 // tm
    nn = N_MLA_HEADS * QK_NOPE_DIM
    return pl.pallas_call(
        functools.partial(_q_kernel, use_rope=use_rope),
        grid=(n_rows // tm,),
        in_specs=[pl.BlockSpec((tm, Q_LORA_RANK), lambda i: (rb0 + i, C_QA // Q_LORA_RANK)),
                  pl.BlockSpec((1, Q_LORA_RANK), lambda i: (0, 0)),
                  pl.BlockSpec(w_q.shape, lambda i: (0, 0)),
                  pl.BlockSpec((tm, LANES), lambda i: (i % (cos2.shape[0] // tm), 0)),
                  pl.BlockSpec((tm, LANES), lambda i: (i % (sin2.shape[0] // tm), 0))],
        out_specs=[pl.BlockSpec((tm, nn), lambda i: (i, 0)),
                   pl.BlockSpec((N_MLA_HEADS, tm, QK_ROPE_DIM), lambda i: (0, i, 0))],
        out_shape=[jax.ShapeDtypeStruct((n_rows, nn), BF16),
                   jax.ShapeDtypeStruct((N_MLA_HEADS, n_rows, QK_ROPE_DIM), BF16)],
        compiler_params=_cp(("parallel",)),
        name="q_proj",
    )(proj, nw, w_q, cos2, sin2)


def _kv_kernel(ckv_ref, kr_ref, nw_ref, w_ref, cos_ref, sin_ref, ckv_o, kr_o, krp_o, kv_o, *, use_rope, normalise):
    x = ckv_ref[...]
    ckv = _rms(x, nw_ref[...]) if normalise else x
    ckv_o[...] = ckv
    kv_o[...] = jnp.dot(ckv.astype(BF16), w_ref[...], preferred_element_type=F32).astype(BF16)
    kr_o[...] = kr_ref[:, :QK_ROPE_DIM]
    if use_rope:
        half = QK_ROPE_DIM // 2
        kr = kr_ref[...]
        lane = lax.broadcasted_iota(I32, kr.shape, 1)
        sw = jnp.where(lane < half, pltpu.roll(kr, LANES - half, 1), pltpu.roll(kr, half, 1))
        krp_o[...] = (kr * cos_ref[...] + sw * sin_ref[...])[:, :QK_ROPE_DIM].astype(BF16)
    else:
        krp_o[...] = kr_ref[:, :QK_ROPE_DIM].astype(BF16)


def _kv_proj(src_ckv, src_kr, nw, w_kv, cos2, sin2, row0, n_rows, cb_ckv, cb_kr, kr_w, use_rope, normalise):
    tm = math.gcd(512, n_rows)
    rb0 = row0 // tm
    nkv = w_kv.shape[1]
    return pl.pallas_call(
        functools.partial(_kv_kernel, use_rope=use_rope, normalise=normalise),
        grid=(n_rows // tm,),
        in_specs=[pl.BlockSpec((tm, KV_LORA_RANK), lambda i: (rb0 + i, cb_ckv)),
                  pl.BlockSpec((tm, kr_w), lambda i: (rb0 + i, cb_kr)),
                  pl.BlockSpec((1, KV_LORA_RANK), lambda i: (0, 0)),
                  pl.BlockSpec(w_kv.shape, lambda i: (0, 0)),
                  pl.BlockSpec((tm, LANES), lambda i: (i % (cos2.shape[0] // tm), 0)),
                  pl.BlockSpec((tm, LANES), lambda i: (i % (sin2.shape[0] // tm), 0))],
        out_specs=[pl.BlockSpec((tm, KV_LORA_RANK), lambda i: (i, 0)),
                   pl.BlockSpec((tm, QK_ROPE_DIM), lambda i: (i, 0)),
                   pl.BlockSpec((tm, QK_ROPE_DIM), lambda i: (i, 0)),
                   pl.BlockSpec((tm, nkv), lambda i: (i, 0))],
        out_shape=[jax.ShapeDtypeStruct((n_rows, KV_LORA_RANK), F32),
                   jax.ShapeDtypeStruct((n_rows, QK_ROPE_DIM), F32),
                   jax.ShapeDtypeStruct((n_rows, QK_ROPE_DIM), BF16),
                   jax.ShapeDtypeStruct((n_rows, nkv), BF16)],
        compiler_params=_cp(("parallel",)),
        name="kv_proj",
    )(src_ckv, src_kr, nw, w_kv, cos2, sin2)


def _attn_kernel(*refs, two_src, scale):
    if two_src:
        qn_ref, qr_ref, k1_ref, v1_ref, r1_ref, k2_ref, v2_ref, r2_ref, o_ref = refs
    else:
        qn_ref, qr_ref, k1_ref, v1_ref, r1_ref, o_ref = refs
    nt = (((1,), (1,)), ((), ()))
    qn = qn_ref[...]
    qr = qr_ref[0]

    def scores(k_ref, r_ref):
        s = lax.dot_general(qn, k_ref[0], nt, preferred_element_type=F32)
        s = s + lax.dot_general(qr, r_ref[0], nt, preferred_element_type=F32)
        return s * scale

    s1 = scores(k1_ref, r1_ref)
    m = jnp.max(s1, axis=-1, keepdims=True)
    if two_src:
        s2 = scores(k2_ref, r2_ref)
        m = jnp.maximum(m, jnp.max(s2, axis=-1, keepdims=True))
    p1 = jnp.exp(s1 - m)
    den = jnp.sum(p1, axis=-1, keepdims=True)
    acc = jnp.dot(p1.astype(BF16), v1_ref[0], preferred_element_type=F32)
    if two_src:
        p2 = jnp.exp(s2 - m)
        den = den + jnp.sum(p2, axis=-1, keepdims=True)
        acc = acc + jnp.dot(p2.astype(BF16), v2_ref[0], preferred_element_type=F32)
    o_ref[...] = (acc / den).astype(BF16)


def _attention(qn, qr, kv1, kr1, kv2, kr2, n_seq, seq_len, tq):
    two_src = kv2 is not None
    nq = seq_len // tq
    s1 = kv1.shape[1]
    scale = (QK_NOPE_DIM + QK_ROPE_DIM) ** -0.5
    in_specs = [pl.BlockSpec((tq, QK_NOPE_DIM), lambda b, h, i: (b * nq + i, h)),
                pl.BlockSpec((1, tq, QK_ROPE_DIM), lambda b, h, i: (h, b * nq + i, 0)),
                pl.BlockSpec((1, s1, QK_NOPE_DIM), lambda b, h, i: (b, 0, 2 * h)),
                pl.BlockSpec((1, s1, V_HEAD_DIM), lambda b, h, i: (b, 0, 2 * h + 1)),
                pl.BlockSpec((1, s1, QK_ROPE_DIM), lambda b, h, i: (b, 0, 0))]
    args = [qn, qr, kv1, kv1, kr1]
    if two_src:
        s2 = kv2.shape[1]
        in_specs += [pl.BlockSpec((1, s2, QK_NOPE_DIM), lambda b, h, i: (b, 0, 2 * h)),
                     pl.BlockSpec((1, s2, V_HEAD_DIM), lambda b, h, i: (b, 0, 2 * h + 1)),
                     pl.BlockSpec((1, s2, QK_ROPE_DIM), lambda b, h, i: (b, 0, 0))]
        args += [kv2, kv2, kr2]
    return pl.pallas_call(
        functools.partial(_attn_kernel, two_src=two_src, scale=scale),
        grid=(n_seq, N_MLA_HEADS, nq),
        in_specs=in_specs,
        out_specs=pl.BlockSpec((tq, V_HEAD_DIM), lambda b, h, i: (b * nq + i, h)),
        out_shape=jax.ShapeDtypeStruct((n_seq * seq_len, N_MLA_HEADS * V_HEAD_DIM), BF16),
        compiler_params=_cp(("parallel", "parallel", "arbitrary")),
        name="attention",
    )(*args)


def _merge_kernel(y_ref, o_ref_, ws_ref, wm_ref, gs_ref, gm_ref, out_ref):
    a = jnp.dot(y_ref[...], ws_ref[...], preferred_element_type=F32)
    b = jnp.dot(o_ref_[...], wm_ref[...], preferred_element_type=F32)
    out_ref[...] = (jax.nn.sigmoid(gs_ref[...]) * a + jax.nn.sigmoid(gm_ref[...]) * b).astype(BF16)


def _merge(y_ssm, o_mla, w_ssm_out, w_mla_out, proj):
    t = y_ssm.shape[0]
    tm, tn = 512, 512
    return pl.pallas_call(
        _merge_kernel,
        grid=(t // tm, D_MODEL // tn),
        in_specs=[pl.BlockSpec((tm, D_SSM), lambda i, j: (i, 0)),
                  pl.BlockSpec((tm, D_MODEL), lambda i, j: (i, 0)),
                  pl.BlockSpec((D_SSM, tn), lambda i, j: (0, j)),
                  pl.BlockSpec((D_MODEL, tn), lambda i, j: (0, j)),
                  pl.BlockSpec((tm, tn), lambda i, j: (i, C_GS // tn + j)),
                  pl.BlockSpec((tm, tn), lambda i, j: (i, C_GM // tn + j))],
        out_specs=pl.BlockSpec((tm, tn), lambda i, j: (i, j)),
        out_shape=jax.ShapeDtypeStruct((t, D_MODEL), BF16),
        compiler_params=_cp(("parallel", "arbitrary")),
        name="merge",
    )(y_ssm, o_mla, w_ssm_out, w_mla_out, proj, proj)


def _layer_norm(x, w, b):
    mu = jnp.mean(x, axis=-1, keepdims=True)
    xc = x - mu
    var = jnp.mean(xc * xc, axis=-1, keepdims=True)
    return xc * lax.rsqrt(var + LN_EPS) * w + b


def _wo_ln_kernel(m_ref, wo_ref, x_ref, g1_ref, lw_ref, lb_ref, sc2_ref, sh2_ref, rw_ref, rb_ref,
                  x1_ref, t2_ref, lg_ref):
    mix = jnp.dot(m_ref[...], wo_ref[...], preferred_element_type=F32)
    x1 = _layer_norm(DEEPNORM_ALPHA * x_ref[...] + g1_ref[0] * mix, lw_ref[...], lb_ref[...])
    x1_ref[...] = x1
    t2 = x1 * (1.0 + sc2_ref[0]) + sh2_ref[0]
    t2_ref[...] = t2
    lg_ref[...] = jnp.dot(t2.astype(BF16), rw_ref[...], preferred_element_type=F32) + rb_ref[...]


def _wo_ln(merged, w_o, x, mods3, ln_w, ln_b, router_w_p, router_b_p, t_ctx, l_lat):
    t = x.shape[0]
    tm = 256
    row = functools.partial(_mod_row, tm=tm, t_ctx=t_ctx, l_lat=l_lat)
    full = lambda i: (0, 0)
    return pl.pallas_call(
        _wo_ln_kernel,
        grid=(t // tm,),
        in_specs=[pl.BlockSpec((tm, D_MODEL), lambda i: (i, 0)),
                  pl.BlockSpec((D_MODEL, D_MODEL), full),
                  pl.BlockSpec((tm, D_MODEL), lambda i: (i, 0)),
                  pl.BlockSpec((1, 1, D_MODEL), lambda i: (row(i), 0, 2)),
                  pl.BlockSpec((1, D_MODEL), full),
                  pl.BlockSpec((1, D_MODEL), full),
                  pl.BlockSpec((1, 1, D_MODEL), lambda i: (row(i), 0, 4)),
                  pl.BlockSpec((1, 1, D_MODEL), lambda i: (row(i), 0, 3)),
                  pl.BlockSpec((D_MODEL, LANES), full),
                  pl.BlockSpec((1, LANES), full)],
        out_specs=[pl.BlockSpec((tm, D_MODEL), lambda i: (i, 0)),
                   pl.BlockSpec((tm, D_MODEL), lambda i: (i, 0)),
                   pl.BlockSpec((tm, LANES), lambda i: (i, 0))],
        out_shape=[jax.ShapeDtypeStruct((t, D_MODEL), F32),
                   jax.ShapeDtypeStruct((t, D_MODEL), F32),
                   jax.ShapeDtypeStruct((t, LANES), F32)],
        compiler_params=_cp(("parallel",)),
        name="wo_ln1",
    )(merged, w_o, x, mods3, ln_w, ln_b, mods3, mods3, router_w_p, router_b_p)


def _router_kernel(lg_ref, ids_ref, wts_ref, rank_ref, cnt_ref, carry):
    i = pl.program_id(0)

    @pl.when(i == 0)
    def _():
        carry[...] = jnp.zeros_like(carry)

    lg = lg_ref[...]
    tm = lg.shape[0]
    lane = lax.broadcasted_iota(I32, lg.shape, 1)
    work = jnp.where(lane < N_EXPERTS, lg, -jnp.inf)
    member = jnp.zeros(lg.shape, F32)
    ids, vals = [], []
    for _ in range(TOP_K):
        m = jnp.max(work, axis=-1, keepdims=True)
        idx = jnp.min(jnp.where(work == m, lane, LANES), axis=-1, keepdims=True)
        sel = lane == idx
        member = jnp.where(sel, 1.0, member)
        work = jnp.where(sel, -jnp.inf, work)
        ids.append(idx)
        vals.append(m)
    ex = [jnp.exp(v - vals[0]) for v in vals]
    den = ex[0] + ex[1] + ex[2] + ex[3]
    r_i = lax.broadcasted_iota(I32, (tm, tm), 0)
    c_i = lax.broadcasted_iota(I32, (tm, tm), 1)
    strict = jnp.where(c_i < r_i, 1.0, 0.0).astype(BF16)
    before = jnp.dot(strict, member.astype(BF16), preferred_element_type=F32) + carry[...]
    ids_o = jnp.zeros(lg.shape, I32)
    wts_o = jnp.zeros(lg.shape, F32)
    rank_o = jnp.zeros(lg.shape, I32)
    for k in range(TOP_K):
        rk = jnp.sum(jnp.where(lane == ids[k], before, 0.0), axis=-1, keepdims=True).astype(I32)
        ids_o = jnp.where(lane == k, ids[k], ids_o)
        wts_o = jnp.where(lane == k, ex[k] / den, wts_o)
        rank_o = jnp.where(lane == k, rk, rank_o)
    ids_ref[...] = ids_o
    wts_ref[...] = wts_o
    rank_ref[...] = rank_o
    carry[...] = carry[...] + jnp.sum(member, axis=0, keepdims=True)
    cnt_ref[...] = carry[...]


def _router(logits):
    t = logits.shape[0]
    tm = 512
    spec = pl.BlockSpec((tm, LANES), lambda i: (i, 0))
    return pl.pallas_call(
        _router_kernel,
        grid=(t // tm,),
        in_specs=[spec],
        out_specs=[spec, spec, spec, pl.BlockSpec((1, LANES), lambda i: (0, 0))],
        out_shape=[jax.ShapeDtypeStruct((t, LANES), I32), jax.ShapeDtypeStruct((t, LANES), F32),
                   jax.ShapeDtypeStruct((t, LANES), I32), jax.ShapeDtypeStruct((1, LANES), F32)],
        scratch_shapes=[pltpu.VMEM((1, LANES), F32)],
        compiler_params=_cp(("arbitrary",)),
        name="router",
    )(logits)


def _pos_kernel(starts_ref, ids_ref, rank_ref, pos_ref):
    ids = ids_ref[...]
    pos = rank_ref[...]
    for e in range(N_EXPERTS):
        pos = pos + jnp.where(ids == e, starts_ref[e], 0)
    pos_ref[...] = pos


def _positions(starts, ids, rank):
    t = ids.shape[0]
    tm = 512
    spec = pl.BlockSpec((tm, LANES), lambda i, s: (i, 0))
    return pl.pallas_call(
        _pos_kernel,
        grid_spec=pltpu.PrefetchScalarGridSpec(
            num_scalar_prefetch=1, grid=(t // tm,), in_specs=[spec, spec], out_specs=spec),
        out_shape=jax.ShapeDtypeStruct((t, LANES), I32),
        compiler_params=_cp(("parallel",)),
        name="positions",
    )(starts, ids, rank)


def _dispatch_kernel(pos_ref, x_ref, zeros_ref, xs_ref, sem, *, tm):
    del zeros_ref
    i = pl.program_id(0)

    def row_copy(r, k):
        dst = pos_ref[(i * tm + r) * TOP_K + k]
        return pltpu.make_async_copy(x_ref.at[pl.ds(r, 1)], xs_ref.at[pl.ds(dst, 1)], sem)

    def issue(r, _):
        for k in range(TOP_K):
            row_copy(r, k).start()
        return 0

    lax.fori_loop(0, tm, issue, 0)

    def drain(r, _):
        for k in range(TOP_K):
            row_copy(r, k).wait()
        return 0

    lax.fori_loop(0, tm, drain, 0)


def _dispatch(pos_flat, t2, n_pad):
    t = t2.shape[0]
    tm = 256
    zeros = jnp.zeros((n_pad, D_MODEL), F32)
    return pl.pallas_call(
        functools.partial(_dispatch_kernel, tm=tm),
        grid_spec=pltpu.PrefetchScalarGridSpec(
            num_scalar_prefetch=1, grid=(t // tm,),
            in_specs=[pl.BlockSpec((tm, D_MODEL), lambda i, p: (i, 0)),
                      pl.BlockSpec(memory_space=pl.ANY)],
            out_specs=pl.BlockSpec(memory_space=pl.ANY),
            scratch_shapes=[pltpu.SemaphoreType.DMA(())]),
        out_shape=jax.ShapeDtypeStruct((n_pad, D_MODEL), F32),
        input_output_aliases={2: 0},
        compiler_params=_cp(("arbitrary",)),
        name="dispatch",
    )(pos_flat, t2, zeros)


def _split_gu_kernel(w_ref, sel_ref, wg_ref, wu_ref):
    w = w_ref[0].astype(BF16)
    sel = sel_ref[...]
    for s in range(w.shape[1] // (2 * LANES)):
        r = jnp.dot(w[:, s * 2 * LANES:(s + 1) * 2 * LANES], sel, preferred_element_type=F32)
        wg_ref[0, :, s * LANES:(s + 1) * LANES] = r[:, :LANES].astype(BF16)
        wu_ref[0, :, s * LANES:(s + 1) * LANES] = r[:, LANES:].astype(BF16)


def _split_gu(w_gu, sel):
    e, k, n2 = w_gu.shape
    tk, tf = 512, 512
    return pl.pallas_call(
        _split_gu_kernel,
        grid=(e, k // tk, n2 // (2 * tf)),
        in_specs=[pl.BlockSpec((1, tk, 2 * tf), lambda a, b, c: (a, b, c)),
                  pl.BlockSpec((2 * LANES, 2 * LANES), lambda a, b, c: (0, 0))],
        out_specs=[pl.BlockSpec((1, tk, tf), lambda a, b, c: (a, b, c)),
                   pl.BlockSpec((1, tk, tf), lambda a, b, c: (a, b, c))],
        out_shape=[jax.ShapeDtypeStruct((e, k, n2 // 2), BF16), jax.ShapeDtypeStruct((e, k, n2 // 2), BF16)],
        compiler_params=_cp(("parallel", "parallel", "parallel")),
        name="split_gu",
    )(w_gu, sel)


def _moe_kernel(te_ref, tv_ref, tb_ref, x_ref, wg_ref, wu_ref, bg_ref, bu_ref, wd_ref, bd_ref, o_ref, x16):
    del te_ref, tb_ref
    i = pl.program_id(0)
    c = pl.program_id(1)

    @pl.when(tv_ref[i] == 1)
    def _():
        @pl.when(c == 0)
        def _():
            x16[...] = x_ref[...].astype(BF16)
            o_ref[...] = jnp.zeros(o_ref.shape, F32) + bd_ref[0]

        x = x16[...]
        gate = jnp.dot(x, wg_ref[0], preferred_element_type=F32) + bg_ref[0]
        up = jnp.dot(x, wu_ref[0], preferred_element_type=F32) + bu_ref[0]
        gate = jnp.minimum(gate, SWIGLU_LIMIT)
        up = jnp.clip(up, -SWIGLU_LIMIT, SWIGLU_LIMIT)
        act = gate * jax.nn.sigmoid(SWIGLU_ALPHA * gate) * (up + 1.0)
        o_ref[...] += jnp.dot(act.astype(BF16), wd_ref[0].astype(BF16), preferred_element_type=F32)

    @pl.when(jnp.logical_and(tv_ref[i] == 0, c == 0))
    def _():
        o_ref[...] = jnp.zeros(o_ref.shape, F32)


MOE_TM = 512
MOE_FC = 512


def _moe(tile_expert, tile_valid, tile_block, xs, wg, wu, bg3, bu3, w_down, bd3):
    n_pad = xs.shape[0]
    n_tiles = n_pad // MOE_TM
    nfc = D_FF // MOE_FC

    def ch(i, c, tv):
        return jnp.where(tv[i] == 1, c, nfc - 1)

    return pl.pallas_call(
        _moe_kernel,
        grid_spec=pltpu.PrefetchScalarGridSpec(
            num_scalar_prefetch=3, grid=(n_tiles, nfc),
            in_specs=[pl.BlockSpec((MOE_TM, D_MODEL), lambda i, c, te, tv, tb: (tb[i], 0)),
                      pl.BlockSpec((1, D_MODEL, MOE_FC), lambda i, c, te, tv, tb: (te[i], 0, ch(i, c, tv))),
                      pl.BlockSpec((1, D_MODEL, MOE_FC), lambda i, c, te, tv, tb: (te[i], 0, ch(i, c, tv))),
                      pl.BlockSpec((1, 1, MOE_FC), lambda i, c, te, tv, tb: (te[i], 0, ch(i, c, tv))),
                      pl.BlockSpec((1, 1, MOE_FC), lambda i, c, te, tv, tb: (te[i], 0, ch(i, c, tv))),
                      pl.BlockSpec((1, MOE_FC, D_MODEL), lambda i, c, te, tv, tb: (te[i], ch(i, c, tv), 0)),
                      pl.BlockSpec((1, 1, D_MODEL), lambda i, c, te, tv, tb: (te[i], 0, 0))],
            out_specs=pl.BlockSpec((MOE_TM, D_MODEL), lambda i, c, te, tv, tb: (i, 0)),
            scratch_shapes=[pltpu.VMEM((MOE_TM, D_MODEL), BF16)]),
        out_shape=jax.ShapeDtypeStruct((n_pad, D_MODEL), F32),
        compiler_params=_cp(("arbitrary", "arbitrary")),
        name="moe_experts",
    )(tile_expert, tile_valid, tile_block, xs, wg, wu, bg3, bu3, w_down, bd3)


def _combine_kernel(pos_ref, ys_ref, w_ref, x1_ref, g2_ref, lw_ref, lb_ref, out_ref, buf, sem, *, tm):
    i = pl.program_id(0)

    def row_copy(r, k):
        src = pos_ref[(i * tm + r) * TOP_K + k]
        return pltpu.make_async_copy(ys_ref.at[pl.ds(src, 1)], buf.at[k, pl.ds(r, 1)], sem)

    def issue(r, _):
        for k in range(TOP_K):
            row_copy(r, k).start()
        return 0

    lax.fori_loop(0, tm, issue, 0)

    def drain(r, _):
        for k in range(TOP_K):
            row_copy(r, k).wait()
        return 0

    lax.fori_loop(0, tm, drain, 0)

    w = w_ref[...]
    moe = buf[0] * w[:, 0:1]
    for k in range(1, TOP_K):
        moe = moe + buf[k] * w[:, k:k + 1]
    out_ref[...] = _layer_norm(DEEPNORM_ALPHA * x1_ref[...] + g2_ref[0] * moe, lw_ref[...], lb_ref[...])


def _combine(pos_flat, ys, wts, x1, mods3, ln_w, ln_b, t_ctx, l_lat):
    t = x1.shape[0]
    tm = 256
    row = functools.partial(_mod_row, tm=tm, t_ctx=t_ctx, l_lat=l_lat)
    return pl.pallas_call(
        functools.partial(_combine_kernel, tm=tm),
        grid_spec=pltpu.PrefetchScalarGridSpec(
            num_scalar_prefetch=1, grid=(t // tm,),
            in_specs=[pl.BlockSpec(memory_space=pl.ANY),
                      pl.BlockSpec((tm, LANES), lambda i, p: (i, 0)),
                      pl.BlockSpec((tm, D_MODEL), lambda i, p: (i, 0)),
                      pl.BlockSpec((1, 1, D_MODEL), lambda i, p: (row(i), 0, 5)),
                      pl.BlockSpec((1, D_MODEL), lambda i, p: (0, 0)),
                      pl.BlockSpec((1, D_MODEL), lambda i, p: (0, 0))],
            out_specs=pl.BlockSpec((tm, D_MODEL), lambda i, p: (i, 0)),
            scratch_shapes=[pltpu.VMEM((TOP_K, tm, D_MODEL), F32), pltpu.SemaphoreType.DMA(())]),
        out_shape=jax.ShapeDtypeStruct((t, D_MODEL), F32),
        compiler_params=_cp(("arbitrary",)),
        name="combine_ln2",
    )(pos_flat, ys, wts, x1, mods3, ln_w, ln_b)


def _dt_perm():
    per_group = [np.concatenate([np.arange(g * HEADS_PER_GROUP, (g + 1) * HEADS_PER_GROUP),
                                 N_SSM_HEADS + np.arange(g * HEADS_PER_GROUP, (g + 1) * HEADS_PER_GROUP)])
                 for g in range(N_SSM_GROUPS)]
    return np.concatenate(per_group)


def _rearrange_w_in(w_in):
    conv_dim = D_SSM + 2 * N_SSM_GROUPS * D_STATE
    o = np.cumsum([0, D_SSM, conv_dim, 2 * N_SSM_HEADS, Q_LORA_RANK, KV_LORA_RANK + QK_ROPE_DIM, 2 * D_MODEL])
    z, xbc, dtw, qa, kva, gl = (w_in[:, o[k]:o[k + 1]] for k in range(6))
    zeros = lambda n: jnp.zeros((w_in.shape[0], n), w_in.dtype)
    parts = [z, xbc, qa, kva[:, :KV_LORA_RANK], kva[:, KV_LORA_RANK:], zeros(LANES - QK_ROPE_DIM),
             dtw[:, _dt_perm()], zeros(C_GS - C_DT - LANES), gl]
    w_r = jnp.concatenate(parts, axis=1).astype(BF16)
    assert w_r.shape[1] == N_PROJ
    return w_r


def _rearrange_w_q(w_q_b):
    w3 = w_q_b.reshape(Q_LORA_RANK, N_MLA_HEADS, QK_NOPE_DIM + QK_ROPE_DIM)
    nope = w3[:, :, :QK_NOPE_DIM].reshape(Q_LORA_RANK, -1)
    rope = w3[:, :, QK_NOPE_DIM:]
    half = QK_ROPE_DIM // 2
    rope_sw = jnp.concatenate([rope[:, :, half:], rope[:, :, :half]], axis=-1)
    return jnp.concatenate([nope, rope.reshape(Q_LORA_RANK, -1), rope_sw.reshape(Q_LORA_RANK, -1)],
                           axis=1).astype(BF16)


def _rope_tables(n_tok):
    rows = n_tok // GRID_W
    row = jnp.repeat(jnp.arange(rows, dtype=F32), GRID_W)
    col = jnp.tile(jnp.arange(GRID_W, dtype=F32), rows)
    n_freq = QK_ROPE_DIM // 4
    inv = ROPE_THETA ** (-jnp.arange(n_freq, dtype=F32) / n_freq)
    ang = jnp.concatenate([row[:, None] * inv, col[:, None] * inv], -1)
    cos, sin = jnp.cos(ang), jnp.sin(ang)
    cos2 = jnp.concatenate([cos, cos, cos, cos], axis=1)
    sin2 = jnp.concatenate([-sin, sin, -sin, sin], axis=1)
    return cos2, sin2


def _gu_selector():
    sel = np.zeros((2 * LANES, 2 * LANES), np.float32)
    j = np.arange(LANES)
    sel[2 * j, j] = 1.0
    sel[2 * j + 1, LANES + j] = 1.0
    return jnp.asarray(sel, BF16)


def kernel(x_prompt, x_sample, cache_mla_ckv, cache_mla_krope, state_ssm, c, c_ctx, w_ada, b_ada, w_in, conv_w,
           conv_b, a_log, dt_bias, d_skip, ssm_norm_w, w_ssm_out, q_a_norm_w, w_q_b, kv_a_norm_w, w_kv_b,
           w_mla_out, w_o, ln1_w, ln1_b, router_w, router_b, w_gu, b_gu, w_down, b_down, ln2_w, ln2_b):
    b_ctx, l_ctx, _ = x_prompt.shape
    b_lat, l_lat, _ = x_sample.shape
    past = cache_mla_ckv.shape[2]
    t_ctx, t_lat = b_ctx * l_ctx, b_lat * l_lat
    t = t_ctx + t_lat
    lyr = 0

    x = jnp.concatenate([x_prompt.reshape(t_ctx, D_MODEL), x_sample.reshape(t_lat, D_MODEL)], axis=0)

    cond8 = jnp.zeros((8, D_MODEL), F32).at[0].set(c_ctx).at[1:1 + b_lat].set(c)
    mods = _adaln(cond8, w_ada[lyr], b_ada[lyr][None, :])
    mods3 = mods.reshape(8, 1, 6 * D_MODEL)

    proj = _inproj(x, mods3, _rearrange_w_in(w_in[lyr]), t_ctx, l_lat)

    xs_c, bc_c = _conv(proj, conv_w[lyr], conv_b[lyr], t_ctx, l_ctx, l_lat)
    perm = _dt_perm()
    dt_bias_r = dt_bias[lyr].reshape(-1)[perm][None, :]
    a_r = (-jnp.exp(a_log[lyr].astype(F32))).reshape(-1)[perm][None, :]
    d_skip_r = jnp.repeat(d_skip[lyr], SSM_HEADDIM)[None, :]
    norm_w = ssm_norm_w[lyr][None, :]
    hp = N_SSM_HEADS * SSM_HEADDIM
    h0_ctx = jnp.zeros((b_ctx, 2, hp, D_STATE), F32)
    h0_lat = state_ssm[:, lyr].reshape(b_lat, 2, hp, D_STATE)
    y_ctx, st_ctx = _ssd(xs_c, bc_c, proj, dt_bias_r, a_r, h0_ctx, d_skip_r, norm_w, 0, b_ctx, l_ctx)
    y_lat, _ = _ssd(xs_c, bc_c, proj, dt_bias_r, a_r, h0_lat, d_skip_r, norm_w, t_ctx, b_lat, l_lat)
    y_ssm = jnp.concatenate([y_ctx, y_lat], axis=0)

    cos2, sin2 = _rope_tables(l_lat)
    w_q = _rearrange_w_q(w_q_b[lyr])
    w_kv = w_kv_b[lyr].astype(BF16)
    qnw = q_a_norm_w[lyr][None, :]
    kvnw = kv_a_norm_w[lyr][None, :]
    qn_c, qr_c = _q_proj(proj, qnw, w_q, cos2, sin2, 0, t_ctx, False)
    qn_l, qr_l = _q_proj(proj, qnw, w_q, cos2, sin2, t_ctx, t_lat, True)
    cb_ckv, cb_kr = C_CKV // KV_LORA_RANK, C_KR // LANES
    ckv_c, kr_c, krp_c, kv_c = _kv_proj(proj, proj, kvnw, w_kv, cos2, sin2, 0, t_ctx, cb_ckv, cb_kr, LANES,
                                        False, True)
    _, _, krp_l, kv_l = _kv_proj(proj, proj, kvnw, w_kv, cos2, sin2, t_ctx, t_lat, cb_ckv, cb_kr, LANES,
                                 True, True)
    cache_ckv = cache_mla_ckv[:, lyr].reshape(b_lat * past, KV_LORA_RANK)
    cache_kr = cache_mla_krope[:, lyr].reshape(b_lat * past, QK_ROPE_DIM)
    _, _, krp_p, kv_p = _kv_proj(cache_ckv, cache_kr, kvnw, w_kv, cos2, sin2, 0, b_lat * past, 0, 0, QK_ROPE_DIM,
                                 False, False)
    nkv = N_MLA_HEADS * (QK_NOPE_DIM + V_HEAD_DIM)
    o_ctx = _attention(qn_c, qr_c, kv_c.reshape(b_ctx, l_ctx, nkv), krp_c.reshape(b_ctx, l_ctx, QK_ROPE_DIM),
                       None, None, b_ctx, l_ctx, min(l_ctx, 256))
    o_lat = _attention(qn_l, qr_l, kv_l.reshape(b_lat, l_lat, nkv), krp_l.reshape(b_lat, l_lat, QK_ROPE_DIM),
                       kv_p.reshape(b_lat, past, nkv), krp_p.reshape(b_lat, past, QK_ROPE_DIM),
                       b_lat, l_lat, min(l_lat, 512))
    o_mla = jnp.concatenate([o_ctx, o_lat], axis=0)

    merged = _merge(y_ssm, o_mla, w_ssm_out[lyr].astype(BF16), w_mla_out[lyr].astype(BF16), proj)

    router_w_p = jnp.zeros((D_MODEL, LANES), F32).at[:, :N_EXPERTS].set(router_w[lyr]).astype(BF16)
    router_b_p = jnp.zeros((1, LANES), F32).at[0, :N_EXPERTS].set(router_b[lyr])
    x1, t2, logits = _wo_ln(merged, w_o[lyr].astype(BF16), x, mods3, ln1_w[lyr][None, :], ln1_b[lyr][None, :],
                            router_w_p, router_b_p, t_ctx, l_lat)

    ids, wts, rank, counts = _router(logits)
    cnt = counts[0, :N_EXPERTS].astype(I32)
    tiles_per = (cnt + MOE_TM - 1) // MOE_TM
    tile_end = jnp.cumsum(tiles_per)
    starts = (tile_end - tiles_per) * MOE_TM
    n_tiles = (t * TOP_K) // MOE_TM + N_EXPERTS
    n_pad = n_tiles * MOE_TM
    tile_idx = jnp.arange(n_tiles, dtype=I32)
    tile_expert_raw = jnp.sum((tile_idx[:, None] >= tile_end[None, :]).astype(I32), axis=1)
    tile_valid = (tile_idx < tile_end[-1]).astype(I32)
    last_expert = jnp.max(jnp.where(tiles_per > 0, jnp.arange(N_EXPERTS, dtype=I32), 0))
    tile_expert = jnp.where(tile_valid == 1, tile_expert_raw, last_expert).astype(I32)
    tile_block = jnp.minimum(tile_idx, tile_end[-1] - 1).astype(I32)
    pos = _positions(starts.astype(I32), ids, rank)
    pos_flat = pos[:, :TOP_K].reshape(-1)

    xs_sorted = _dispatch(pos_flat, t2, n_pad)
    wg, wu = _split_gu(w_gu[lyr], _gu_selector())
    bg3 = b_gu[lyr][:, 0::2][:, None, :]
    bu3 = b_gu[lyr][:, 1::2][:, None, :]
    ys = _moe(tile_expert, tile_valid, tile_block, xs_sorted, wg, wu, bg3, bu3, w_down[lyr], b_down[lyr][:, None, :])
    out = _combine(pos_flat, ys, wts, x1, mods3, ln2_w[lyr][None, :], ln2_b[lyr][None, :], t_ctx, l_lat)

    y_p = out[:t_ctx].reshape(b_ctx, l_ctx, D_MODEL)
    y_s = out[t_ctx:].reshape(b_lat, l_lat, D_MODEL)
    new_ckv = ckv_c.reshape(b_ctx, 1, l_ctx, KV_LORA_RANK)
    new_krope = kr_c.reshape(b_ctx, 1, l_ctx, QK_ROPE_DIM)
    new_state = st_ctx.reshape(b_ctx, 1, 2, N_SSM_HEADS, SSM_HEADDIM, D_STATE)
    return (y_p, y_s, new_ckv, new_krope, new_state)
```
